```python
import jax, jax.numpy as jnp
from jax import lax
import numpy as np

D_MODEL = 4096
BATCH = 2
SEQ = 4096
DEPTH = 2

CTX_LEN = 256
GRID_W = 64
N_MOD = 6
NORM_EPS = 1e-6

LRU_WIDTH = D_MODEL // 2
LRU_BLOCKS = 16
LRU_BLOCK = LRU_WIDTH // LRU_BLOCKS
CONV_WIDTH = 4
CONV_PAD_LEFT = 2
LRU_C = 8.0
LRU_MIN_RAD = 0.9
LRU_MAX_RAD = 0.999

RWKV_WIDTH = D_MODEL // 2
RWKV_HEAD = 64
RWKV_HEADS = RWKV_WIDTH // RWKV_HEAD
DECAY_LORA = 128
ICL_LORA = 128
GATE_LORA = 480
RWKV_GN_EPS = 64e-5
EVEN_IN = 2 * LRU_WIDTH + 3 * RWKV_WIDTH + DECAY_LORA + ICL_LORA + GATE_LORA

GLA_HEADS = 8
GLA_KEY = D_MODEL // 2
GLA_VAL = D_MODEL
GLA_DK = GLA_KEY // GLA_HEADS
GLA_DV = GLA_VAL // GLA_HEADS
GLA_GATE_LORA = 16
GLA_GATE_NORM = 16.0
GLA_CLAMP_MIN = -1.0
GLA_CHUNK = 64
GLA_EPS = 1e-5
ODD_IN = 2 * GLA_KEY + 2 * GLA_VAL + 2 * GLA_GATE_LORA

N_EXPERTS = 32
TOP_K = 4
D_EXPERT = D_MODEL // 8
SWIGLU_LIMIT = 7.0
SWIGLU_ALPHA = 1.702
MOE_BLOCK = 128

kernel_name = 'hybrid_lru_rwkv7_gla_moe_dit'


def rms_norm(z, gain):
    z32 = z.astype(jnp.float32)
    z32 = z32 * lax.rsqrt(jnp.mean(z32 * z32, axis=-1, keepdims=True) + NORM_EPS)
    return (z32 * gain.astype(jnp.float32)).astype(z.dtype)


def modulate(z, shift, scale):
    return z * (1.0 + scale) + shift


def per_segment(fn, z, n_ctx):
    return jnp.concatenate([fn(z[:, :n_ctx]), fn(z[:, n_ctx:])], axis=1)


def seg_reverse(z, n_ctx):
    return per_segment(lambda s: jnp.flip(s, axis=1), z, n_ctx)


def dir_pair(z, n_ctx):
    return jnp.stack([z, seg_reverse(z, n_ctx)], axis=0)


def dir_align(z2, n_ctx):
    return jnp.stack([z2[0], seg_reverse(z2[1], n_ctx)], axis=0)


def to_column_major(z, rows):
    b, _, d = z.shape
    return z.reshape(b, rows, GRID_W, d).transpose(0, 2, 1, 3).reshape(b, rows * GRID_W, d)


def from_column_major(z, rows):
    b, _, d = z.shape
    return z.reshape(b, GRID_W, rows, d).transpose(0, 2, 1, 3).reshape(b, rows * GRID_W, d)


def depthwise_conv(z, w, pad_left):
    width, ch = w.shape
    return lax.conv_general_dilated(z, w[:, None, :], window_strides=(1,),
                                    padding=[(pad_left, width - 1 - pad_left)],
                                    dimension_numbers=('NWC', 'WIO', 'NWC'),
                                    feature_group_count=ch)


def centred_token_shift(z, mu):
    zp = jnp.pad(z, ((0, 0), (1, 1), (0, 0)))
    return z + (0.5 * (zp[:, :-2] + zp[:, 2:]) - z) * mu


def linear_scan(a, b, axis):
    def combine(lhs, rhs):
        return lhs[0] * rhs[0], rhs[0] * lhs[1] + rhs[1]
    return lax.associative_scan(combine, (a, b), axis=axis)[1]


def rwkv7_scan(r, w, k, v, a, b):
    def step(state, inp):
        r_t, w_t, k_t, v_t, a_t, b_t = inp
        sa = jnp.einsum('dbhij,dbhj->dbhi', state, a_t)
        state = (state * w_t[..., None, :] + sa[..., :, None] * b_t[..., None, :]
                 + v_t[..., :, None] * k_t[..., None, :])
        return state, jnp.einsum('dbhij,dbhj->dbhi', state, r_t)
    s0 = jnp.zeros(r.shape[1:] + (r.shape[-1],), jnp.float32)
    _, y = lax.scan(step, s0, (r, w, k, v, a, b))
    return y


def gla_chunked(q, k, v, log_alpha):
    nd, bsz, t, nh, dk = q.shape
    dv = v.shape[-1]
    nc = t // GLA_CHUNK
    chunk = lambda z: z.reshape(nd, bsz, nc, GLA_CHUNK, nh, z.shape[-1])
    q, k, v, log_alpha = chunk(q), chunk(k), chunk(v), chunk(log_alpha)
    cum = jnp.cumsum(log_alpha, axis=3)
    cum_last = cum[:, :, :, -1]
    q_dec = q * jnp.exp(cum)
    k_inv = k * jnp.exp(-cum)
    k_end = k * jnp.exp(cum_last[:, :, :, None] - cum)
    lower = jnp.tril(jnp.ones((GLA_CHUNK, GLA_CHUNK), dtype=bool))
    scores = jnp.where(lower, jnp.einsum('dbnihk,dbnjhk->dbnhij', q_dec, k_inv), 0.0)
    o_intra = jnp.einsum('dbnhij,dbnjhv->dbnihv', scores, v)

    def step(state, inp):
        q_n, k_n, v_n, dec_n = inp
        o_n = jnp.einsum('dbihk,dbhkv->dbihv', q_n, state)
        state = state * dec_n[..., None] + jnp.einsum('dbjhk,dbjhv->dbhkv', k_n, v_n)
        return state, o_n
    s0 = jnp.zeros((nd, bsz, nh, dk, dv), jnp.float32)
    xs = tuple(jnp.moveaxis(z, 2, 0) for z in (q_dec, k_end, v, jnp.exp(cum_last)))
    _, o_inter = lax.scan(step, s0, xs)
    o = o_intra + jnp.moveaxis(o_inter, 0, 2)
    return o.reshape(nd, bsz, t, nh, dv)


def rwkv_heads(z):
    return z.reshape(z.shape[:-1] + (RWKV_HEADS, RWKV_HEAD))


def even_mixer(h, n_ctx, w_in, w_out, conv_w, conv_b, ga_w, ga_b, gx_w, gx_b, lam,
               mu, w0, w2, a0, a2, g2, k_k, k_a, r_k, ln_w, ln_b):
    bsz, t, _ = h.shape
    f32 = jnp.float32
    proj = h @ w_in
    gate_br, xa, rwkv_in = jnp.split(proj, [LRU_WIDTH, 2 * LRU_WIDTH], axis=-1)

    xa = per_segment(lambda z: depthwise_conv(z, conv_w, CONV_PAD_LEFT), xa, n_ctx) + conv_b
    xa2 = dir_pair(xa, n_ctx)
    blocks = xa2.reshape(2, bsz, t, LRU_BLOCKS, LRU_BLOCK)
    r_gate = jax.nn.sigmoid(jnp.einsum('dbtnk,dnkj->dbtnj', blocks, ga_w).reshape(xa2.shape) + ga_b[:, None, None])
    i_gate = jax.nn.sigmoid(jnp.einsum('dbtnk,dnkj->dbtnj', blocks, gx_w).reshape(xa2.shape) + gx_b[:, None, None])
    log_a = -LRU_C * r_gate.astype(f32) * jax.nn.softplus(-lam.astype(f32))[:, None, None]
    lru_in = jnp.sqrt(-jnp.expm1(2.0 * log_a)) * (i_gate * xa2).astype(f32)
    hs = dir_align(linear_scan(jnp.exp(log_a), lru_in, axis=2), n_ctx)
    y_a = (hs[0] + hs[1]).astype(h.dtype) * jax.nn.gelu(gate_br)

    rwkv_in = per_segment(lambda z: centred_token_shift(z, mu), rwkv_in, n_ctx)
    r, k, v, w_lo, a_lo, g_lo = jnp.split(
        rwkv_in, [RWKV_WIDTH, 2 * RWKV_WIDTH, 3 * RWKV_WIDTH,
                  3 * RWKV_WIDTH + DECAY_LORA, 3 * RWKV_WIDTH + DECAY_LORA + ICL_LORA], axis=-1)
    g = jax.nn.sigmoid(g_lo) @ g2
    kk = rwkv_heads((k * k_k).astype(f32))
    kk = kk / jnp.maximum(jnp.linalg.norm(kk, axis=-1, keepdims=True), 1e-12)
    w_log = -jax.nn.softplus(-(w0[:, None, None] + jnp.einsum('btr,drc->dbtc', jnp.tanh(w_lo), w2)).astype(f32)) - 0.5
    a_icl = jax.nn.sigmoid((a0[:, None, None] + jnp.einsum('btr,drc->dbtc', a_lo, a2)).astype(f32))
    k_dir = rwkv_heads(k.astype(f32) * (1.0 + (a_icl - 1.0) * k_a.astype(f32)))
    r_h = rwkv_heads(r.astype(f32))
    v_h = rwkv_heads(v.astype(f32))
    scan_in = (dir_pair(r_h, n_ctx),
               dir_align(rwkv_heads(jnp.exp(-jnp.exp(w_log))), n_ctx),
               dir_align(k_dir, n_ctx),
               dir_pair(v_h, n_ctx),
               dir_pair(-kk, n_ctx),
               dir_align(kk[None] * rwkv_heads(a_icl), n_ctx))
    y = rwkv7_scan(*[jnp.moveaxis(z, 2, 0) for z in scan_in])
    y = dir_align(jnp.moveaxis(y, 0, 2), n_ctx)
    y = y[0] + y[1]
    mean = jnp.mean(y, axis=-1, keepdims=True)
    var = jnp.mean(jnp.square(y - mean), axis=-1, keepdims=True)
    yn = ((y - mean) * lax.rsqrt(var + RWKV_GN_EPS)).reshape(bsz, t, RWKV_WIDTH) * ln_w + ln_b
    bonus = jnp.sum(jnp.sum(r_h * k_dir * r_k, axis=-1, keepdims=True), axis=0) * v_h
    y_b = (yn + bonus.reshape(bsz, t, RWKV_WIDTH)).astype(h.dtype) * g

    return jnp.concatenate([y_a, y_b], axis=-1) @ w_out


def odd_mixer(h, n_ctx, w_in, w_out, gk_w2, gk_b, norm_w):
    bsz, t, _ = h.shape
    f32 = jnp.float32
    proj = h @ w_in
    q, k, v, gate, gk_lo = jnp.split(
        proj, [GLA_KEY, 2 * GLA_KEY, 2 * GLA_KEY + GLA_VAL, 2 * GLA_KEY + 2 * GLA_VAL], axis=-1)
    gk = jnp.einsum('btdr,drc->dbtc', gk_lo.reshape(bsz, t, 2, GLA_GATE_LORA), gk_w2) + gk_b[:, None, None]
    log_alpha = jnp.maximum(jax.nn.log_sigmoid(gk.astype(f32)) / GLA_GATE_NORM, GLA_CLAMP_MIN)
    gla_heads = lambda z: z.reshape(z.shape[:-1] + (GLA_HEADS, -1)).astype(f32)
    o = gla_chunked(dir_pair(gla_heads(q) * GLA_DK ** -0.5, n_ctx),
                    dir_pair(gla_heads(k), n_ctx),
                    dir_pair(gla_heads(v), n_ctx),
                    dir_align(gla_heads(log_alpha), n_ctx))
    o = dir_align(o, n_ctx)
    o = o[0] + o[1]
    o = o * lax.rsqrt(jnp.mean(o * o, axis=-1, keepdims=True) + GLA_EPS) * norm_w
    o = o.reshape(bsz, t, GLA_VAL).astype(h.dtype) * jax.nn.silu(gate)
    return o @ w_out


def moe_ffn(h, router_w, router_b, w_gu, b_gu, w_down, b_down):
    n, d = h.shape
    logits = (h @ router_w + router_b).astype(jnp.float32)
    top_val, top_idx = lax.top_k(logits, TOP_K)
    gates = jax.nn.softmax(top_val, axis=-1)
    flat_e = top_idx.reshape(-1)
    flat_tok = jnp.repeat(jnp.arange(n, dtype=jnp.int32), TOP_K)
    order = jnp.argsort(flat_e)
    e_sorted = flat_e[order]
    counts = jnp.zeros((N_EXPERTS,), jnp.int32).at[flat_e].add(1)
    padded = (counts + MOE_BLOCK - 1) // MOE_BLOCK * MOE_BLOCK
    pad_end = jnp.cumsum(padded)
    raw_start = jnp.cumsum(counts) - counts
    dest = (pad_end - padded)[e_sorted] + jnp.arange(n * TOP_K, dtype=jnp.int32) - raw_start[e_sorted]
    n_blocks = -(-(n * TOP_K) // MOE_BLOCK) + N_EXPERTS
    row_tok = jnp.full((n_blocks * MOE_BLOCK,), n, jnp.int32).at[dest].set(flat_tok[order])
    row_gate = jnp.zeros((n_blocks * MOE_BLOCK,), jnp.float32).at[dest].set(gates.reshape(-1)[order])
    block_start = jnp.arange(n_blocks, dtype=jnp.int32) * MOE_BLOCK
    block_e = jnp.minimum(jnp.sum(block_start[:, None] >= pad_end[None, :], axis=1), N_EXPERTS - 1)

    def expert_block(args):
        tok, e, gate = args
        xb = jnp.take(h, tok, axis=0, mode='fill', fill_value=0)
        glu, lin = jnp.split(xb @ w_gu[e] + b_gu[e], 2, axis=-1)
        glu = jnp.minimum(glu, SWIGLU_LIMIT)
        lin = jnp.clip(lin, -SWIGLU_LIMIT, SWIGLU_LIMIT)
        act = glu * jax.nn.sigmoid(SWIGLU_ALPHA * glu) * (lin + 1.0)
        return (act @ w_down[e] + b_down[e]) * gate[:, None].astype(h.dtype)

    y = lax.map(expert_block, (row_tok.reshape(n_blocks, MOE_BLOCK), block_e,
                               row_gate.reshape(n_blocks, MOE_BLOCK)))
    return jax.ops.segment_sum(y.reshape(-1, d), row_tok, num_segments=n)


def setup_inputs(seed: int = 0) -> dict:
    key = jax.random.key(seed)
    ks = iter(jax.random.split(key, 48))
    f32 = jnp.float32
    n_even = (DEPTH + 1) // 2
    n_odd = DEPTH // 2

    def nrm(shape, scale):
        return jax.random.normal(next(ks), shape, f32) * scale

    def unif(shape, lo, hi):
        return jax.random.uniform(next(ks), shape, f32, lo, hi)

    def gain(shape):
        return 1.0 + nrm(shape, 0.05)

    def lam_init(shape):
        u = unif(shape, LRU_MIN_RAD, LRU_MAX_RAD)
        return jnp.log(u) - jnp.log1p(-u)

    return {
        'x': nrm((BATCH, SEQ, D_MODEL), 1.0),
        'c': nrm((BATCH, D_MODEL), 1.0),
        'ctx': nrm((BATCH, CTX_LEN, D_MODEL), 1.0),
        'c_ctx': nrm((D_MODEL,), 1.0),
        'ada_w': nrm((DEPTH, D_MODEL, N_MOD * D_MODEL), 0.5 * D_MODEL ** -0.5),
        'ada_b': nrm((DEPTH, N_MOD * D_MODEL), 0.02),
        'norm_mix': gain((DEPTH, D_MODEL)),
        'norm_ffn': gain((DEPTH, D_MODEL)),
        'norm_final': gain((D_MODEL,)),
        'ev_w_in': nrm((n_even, D_MODEL, EVEN_IN), D_MODEL ** -0.5),
        'ev_w_out': nrm((n_even, LRU_WIDTH + RWKV_WIDTH, D_MODEL), (LRU_WIDTH + RWKV_WIDTH) ** -0.5),
        'lru_conv_w': nrm((n_even, CONV_WIDTH, LRU_WIDTH), CONV_WIDTH ** -0.5),
        'lru_conv_b': nrm((n_even, LRU_WIDTH), 0.02),
        'lru_gate_a_w': nrm((n_even, 2, LRU_BLOCKS, LRU_BLOCK, LRU_BLOCK), LRU_BLOCK ** -0.5),
        'lru_gate_a_b': nrm((n_even, 2, LRU_WIDTH), 0.1),
        'lru_gate_x_w': nrm((n_even, 2, LRU_BLOCKS, LRU_BLOCK, LRU_BLOCK), LRU_BLOCK ** -0.5),
        'lru_gate_x_b': nrm((n_even, 2, LRU_WIDTH), 0.1),
        'lru_lambda': lam_init((n_even, 2, LRU_WIDTH)),
        'rwkv_mu': unif((n_even, EVEN_IN - 2 * LRU_WIDTH), 0.0, 1.0),
        'rwkv_w0': unif((n_even, 2, RWKV_WIDTH), -6.0, -1.0),
        'rwkv_w2': nrm((n_even, 2, DECAY_LORA, RWKV_WIDTH), 0.5 * DECAY_LORA ** -0.5),
        'rwkv_a0': nrm((n_even, 2, RWKV_WIDTH), 0.1),
        'rwkv_a2': nrm((n_even, 2, ICL_LORA, RWKV_WIDTH), 0.5 * ICL_LORA ** -0.5),
        'rwkv_g2': nrm((n_even, GATE_LORA, RWKV_WIDTH), GATE_LORA ** -0.5),
        'rwkv_k_k': 0.85 + nrm((n_even, RWKV_WIDTH), 0.05),
        'rwkv_k_a': 1.0 + nrm((n_even, RWKV_WIDTH), 0.05),
        'rwkv_r_k': nrm((n_even, RWKV_HEADS, RWKV_HEAD), 0.1),
        'rwkv_ln_w': gain((n_even, RWKV_WIDTH)),
        'rwkv_ln_b': nrm((n_even, RWKV_WIDTH), 0.02),
        'od_w_in': nrm((n_odd, D_MODEL, ODD_IN), D_MODEL ** -0.5),
        'od_w_out': nrm((n_odd, GLA_VAL, D_MODEL), GLA_VAL ** -0.5),
        'gla_gk_w2': nrm((n_odd, 2, GLA_GATE_LORA, GLA_KEY), GLA_GATE_LORA ** -0.5),
        'gla_gk_b': nrm((n_odd, 2, GLA_KEY), 0.1),
        'gla_norm_w': gain((n_odd, GLA_DV)),
        'router_w': nrm((DEPTH, D_MODEL, N_EXPERTS), D_MODEL ** -0.5),
        'router_b': nrm((DEPTH, N_EXPERTS), 0.01),
        'exp_w_gu': nrm((DEPTH, N_EXPERTS, D_MODEL, 2 * D_EXPERT), D_MODEL ** -0.5),
        'exp_b_gu': nrm((DEPTH, N_EXPERTS, 2 * D_EXPERT), 0.02),
        'exp_w_down': nrm((DEPTH, N_EXPERTS, D_EXPERT, D_MODEL), D_EXPERT ** -0.5),
        'exp_b_down': nrm((DEPTH, N_EXPERTS, D_MODEL), 0.02),
    }


def reference(x, c, ctx, c_ctx, ada_w, ada_b, norm_mix, norm_ffn, norm_final,
              ev_w_in, ev_w_out, lru_conv_w, lru_conv_b, lru_gate_a_w, lru_gate_a_b,
              lru_gate_x_w, lru_gate_x_b, lru_lambda,
              rwkv_mu, rwkv_w0, rwkv_w2, rwkv_a0, rwkv_a2, rwkv_g2, rwkv_k_k, rwkv_k_a,
              rwkv_r_k, rwkv_ln_w, rwkv_ln_b,
              od_w_in, od_w_out, gla_gk_w2, gla_gk_b, gla_norm_w,
              router_w, router_b, exp_w_gu, exp_b_gu, exp_w_down, exp_b_down):
    bsz, seq, d = x.shape
    n_ctx = ctx.shape[1]
    rows = seq // GRID_W
    cond_lat = jax.nn.silu(c)
    cond_ctx = jax.nn.silu(c_ctx)
    for i in range(DEPTH):
        last = i == DEPTH - 1
        j = i // 2
        mod_lat = jnp.split((cond_lat @ ada_w[i] + ada_b[i])[:, None, :], N_MOD, axis=-1)
        mod_ctx = jnp.split(cond_ctx @ ada_w[i] + ada_b[i], N_MOD, axis=-1)

        h_lat = modulate(rms_norm(x, norm_mix[i]), mod_lat[0], mod_lat[1])
        h_ctx = modulate(rms_norm(ctx, norm_mix[i]), mod_ctx[0], mod_ctx[1])
        if i % 2 == 0:
            y = even_mixer(jnp.concatenate([h_ctx, h_lat], axis=1), n_ctx,
                           ev_w_in[j], ev_w_out[j], lru_conv_w[j], lru_conv_b[j],
                           lru_gate_a_w[j], lru_gate_a_b[j], lru_gate_x_w[j], lru_gate_x_b[j],
                           lru_lambda[j], rwkv_mu[j], rwkv_w0[j], rwkv_w2[j], rwkv_a0[j],
                           rwkv_a2[j], rwkv_g2[j], rwkv_k_k[j], rwkv_k_a[j], rwkv_r_k[j],
                           rwkv_ln_w[j], rwkv_ln_b[j])
            y_ctx, y_lat = y[:, :n_ctx], y[:, n_ctx:]
        else:
            y = odd_mixer(jnp.concatenate([h_ctx, to_column_major(h_lat, rows)], axis=1), n_ctx,
                          od_w_in[j], od_w_out[j], gla_gk_w2[j], gla_gk_b[j], gla_norm_w[j])
            y_ctx, y_lat = y[:, :n_ctx], from_column_major(y[:, n_ctx:], rows)
        x = x + mod_lat[2] * y_lat

        f_lat = modulate(rms_norm(x, norm_ffn[i]), mod_lat[3], mod_lat[4])
        moe_args = (router_w[i], router_b[i], exp_w_gu[i], exp_b_gu[i], exp_w_down[i], exp_b_down[i])
        if last:
            f = moe_ffn(f_lat.reshape(-1, d), *moe_args).reshape(bsz, seq, d)
            x = x + mod_lat[5] * f
        else:
            ctx = ctx + mod_ctx[2] * y_ctx
            f_ctx = modulate(rms_norm(ctx, norm_ffn[i]), mod_ctx[3], mod_ctx[4])
            f = moe_ffn(jnp.concatenate([f_ctx, f_lat], axis=1).reshape(-1, d), *moe_args)
            f = f.reshape(bsz, n_ctx + seq, d)
            ctx = ctx + mod_ctx[5] * f[:, :n_ctx]
            x = x + mod_lat[5] * f[:, n_ctx:]
    return rms_norm(x, norm_final)
```

```python
import functools

import jax
import jax.numpy as jnp
import numpy as np
from jax import lax
from jax.experimental import pallas as pl
from jax.experimental.pallas import tpu as pltpu

F32 = jnp.float32
BF16 = jnp.bfloat16
HI = lax.Precision.HIGHEST

GRID_W = 64
NORM_EPS = 1e-6
N_MOD = 6
LRU_BLOCKS = 16
LRU_BLOCK = 128
LRU_C = 8.0
RWKV_HEADS = 32
RWKV_HEAD = 64
RWKV_GN_EPS = 64e-5
GLA_HEADS = 8
GLA_GATE_LORA = 16
GLA_GATE_NORM = 16.0
GLA_CLAMP_MIN = -1.0
GLA_EPS = 1e-5
TOP_K = 4
SWIGLU_LIMIT = 7.0
SWIGLU_ALPHA = 1.702
MOE_BLOCK = 128
CHUNK = 64
VMEM_LIMIT = 56 * 1024 * 1024


def _cparams(sem):
    return pltpu.CompilerParams(dimension_semantics=sem, vmem_limit_bytes=VMEM_LIMIT)


def _bdot(a, b):
    return jnp.dot(a.astype(BF16), b.astype(BF16), preferred_element_type=F32)


def _bdot_nt(a, b):
    return lax.dot_general(a.astype(BF16), b.astype(BF16), (((1,), (1,)), ((), ())),
                           preferred_element_type=F32)


def _bdot_tn(a, b):
    return lax.dot_general(a.astype(BF16), b.astype(BF16), (((0,), (0,)), ((), ())),
                           preferred_element_type=F32)


def _hdot(a, b):
    return jnp.dot(a, b, preferred_element_type=F32, precision=HI)


def _sigmoid(x):
    return 1.0 / (1.0 + jnp.exp(-x))


def _softplus(x):
    return jnp.maximum(x, 0.0) + jnp.log(1.0 + jnp.exp(-jnp.abs(x)))


def _scan_block(p, n_ctx_blocks, n_lat_blocks, reverse):
    if not reverse:
        return jnp.where(p < n_ctx_blocks, n_lat_blocks + p, p - n_ctx_blocks)
    return jnp.where(p < n_ctx_blocks, n_lat_blocks + n_ctx_blocks - 1 - p,
                     n_lat_blocks - 1 - (p - n_ctx_blocks))


def _mod_body(c_ref, w_ref, b_ref, o_ref):
    c = c_ref[...]
    cond = c * _sigmoid(c)
    o_ref[...] = _bdot(cond, w_ref[...]) + b_ref[...]


def ada_mod(cond_in, w, b):
    d, n = w.shape
    tn = 512
    return pl.pallas_call(
        _mod_body,
        grid=(n // tn,),
        in_specs=[pl.BlockSpec((8, d), lambda j: (0, 0)),
                  pl.BlockSpec((d, tn), lambda j: (0, j)),
                  pl.BlockSpec((1, tn), lambda j: (0, j))],
        out_specs=pl.BlockSpec((8, tn), lambda j: (0, j)),
        out_shape=jax.ShapeDtypeStruct((8, n), F32),
        compiler_params=_cparams(("parallel",)),
        name="ada_mod",
    )(cond_in, w, b.reshape(1, n))


def _mm_body(x_ref, w_ref, o_ref, *, act):
    x = x_ref[...]
    if act == "sigmoid":
        x = _sigmoid(x)
    o_ref[...] = _bdot(x, w_ref[...]).astype(o_ref.dtype)


def matmul(x, w, *, n_cols=None, tm=512, tn=256, act=None, out_dtype=F32):
    m, k = x.shape
    n = w.shape[1] if n_cols is None else n_cols
    tn = min(tn, n)
    while m % tm:
        tm //= 2
    assert tm % 8 == 0 and n % tn == 0
    return pl.pallas_call(
        functools.partial(_mm_body, act=act),
        grid=(m // tm, n // tn),
        in_specs=[pl.BlockSpec((tm, k), lambda i, j: (i, 0)),
                  pl.BlockSpec((k, tn), lambda i, j: (0, j))],
        out_specs=pl.BlockSpec((tm, tn), lambda i, j: (i, j)),
        out_shape=jax.ShapeDtypeStruct((m, n), out_dtype),
        compiler_params=_cparams(("parallel", "parallel")),
        name="matmul",
    )(x, w)


def _rms_mod(x, gain, shift, scale):
    z = x * lax.rsqrt(jnp.mean(x * x, axis=-1, keepdims=True) + NORM_EPS) * gain
    return z * (1.0 + scale) + shift


def _normmod_body(xl_ref, xc_ref, gain_ref, sh_ref, sc_ref, h_ref, *, n_lat_blocks):
    is_ctx = pl.program_id(1) >= n_lat_blocks
    x = jnp.where(is_ctx, xc_ref[0], xl_ref[0])
    h_ref[0] = _rms_mod(x, gain_ref[...], sh_ref[0], sc_ref[0])


def _resid_normmod_body(xl_ref, xc_ref, y_ref, gate_ref, gain_ref, sh_ref, sc_ref,
                        xln_ref, *rest, n_lat_blocks, with_ctx):
    f_ref = rest[-1]
    is_ctx = pl.program_id(1) >= n_lat_blocks
    x = jnp.where(is_ctx, xc_ref[0], xl_ref[0]) + gate_ref[0] * y_ref[0]
    f_ref[0] = _rms_mod(x, gain_ref[...], sh_ref[0], sc_ref[0])
    if with_ctx:
        xcn_ref = rest[0]

        @pl.when(is_ctx)
        def _():
            xcn_ref[0] = x

        @pl.when(jnp.logical_not(is_ctx))
        def _():
            xln_ref[0] = x
    else:
        xln_ref[0] = x


def _token_specs(bsz, seq, n_ctx, d, rb, col_major):
    nl, nc = seq // rb, n_ctx // rb
    if col_major:
        assert rb == seq // GRID_W
        lat_spec = pl.BlockSpec((1, rb, d), lambda b, t, *_: (b, 0, jnp.minimum(t, nl - 1)))
    else:
        lat_spec = pl.BlockSpec((1, rb, d), lambda b, t, *_: (b, jnp.minimum(t, nl - 1), 0))
    ctx_spec = pl.BlockSpec((1, rb, d), lambda b, t, *_: (b, jnp.maximum(t - nl, 0), 0))
    return nl, nc, lat_spec, ctx_spec


def _mod_spec(d, nl):
    return pl.BlockSpec((1, 1, d), lambda b, t, *_: (jnp.where(t >= nl, 2, b), 0, 0))


def norm_mod(x_lat, x_ctx, gain, shift, scale, *, rb, col_major=False, with_ctx=True):
    bsz, seq, d = x_lat.shape
    n_ctx = x_ctx.shape[1]
    nl, nc, lat_spec, ctx_spec = _token_specs(bsz, seq, n_ctx, d, rb, col_major)
    nt = nl + (nc if with_ctx else 0)
    xl = x_lat.reshape(bsz, seq // GRID_W, GRID_W * d) if col_major else x_lat
    return pl.pallas_call(
        functools.partial(_normmod_body, n_lat_blocks=nl),
        grid=(bsz, nt),
        in_specs=[lat_spec, ctx_spec, pl.BlockSpec((1, d), lambda b, t: (0, 0)),
                  _mod_spec(d, nl), _mod_spec(d, nl)],
        out_specs=pl.BlockSpec((1, rb, d), lambda b, t: (b, t, 0)),
        out_shape=jax.ShapeDtypeStruct((bsz, nt * rb, d), F32),
        compiler_params=_cparams(("parallel", "arbitrary")),
        name="norm_mod",
    )(xl, x_ctx, gain.reshape(1, d), shift, scale)


def resid_norm_mod(x_lat, x_ctx, y, gate, gain, shift, scale, *, rb, col_major=False, with_ctx=True):
    bsz, seq, d = x_lat.shape
    n_ctx = x_ctx.shape[1]
    nl, nc, lat_spec, ctx_spec = _token_specs(bsz, seq, n_ctx, d, rb, col_major)
    nt = nl + (nc if with_ctx else 0)
    xl = x_lat.reshape(bsz, seq // GRID_W, GRID_W * d) if col_major else x_lat
    stream_spec = pl.BlockSpec((1, rb, d), lambda b, t: (b, t, 0))
    out_specs = [lat_spec] + ([ctx_spec] if with_ctx else []) + [stream_spec]
    out_shape = ([jax.ShapeDtypeStruct(xl.shape, F32)]
                 + ([jax.ShapeDtypeStruct(x_ctx.shape, F32)] if with_ctx else [])
                 + [jax.ShapeDtypeStruct((bsz, nt * rb, d), F32)])
    outs = pl.pallas_call(
        functools.partial(_resid_normmod_body, n_lat_blocks=nl, with_ctx=with_ctx),
        grid=(bsz, nt),
        in_specs=[lat_spec, ctx_spec, stream_spec, _mod_spec(d, nl),
                  pl.BlockSpec((1, d), lambda b, t: (0, 0)), _mod_spec(d, nl), _mod_spec(d, nl)],
        out_specs=out_specs,
        out_shape=out_shape,
        compiler_params=_cparams(("parallel", "arbitrary")),
        name="resid_norm_mod",
    )(xl, x_ctx, y, gate, gain.reshape(1, d), shift, scale)
    xcn = outs[1] if with_ctx else None
    return outs[0].reshape(x_lat.shape), xcn, outs[-1]


def _local_mix_body(cur_ref, prev_ref, next_ref, coef_ref, bias_ref, o_ref, ext_ref, *, tb, n_lat_blocks, n_blocks):
    t = pl.program_id(1)
    has_prev = jnp.logical_and(t != 0, t != n_lat_blocks)
    has_next = jnp.logical_and(t != n_lat_blocks - 1, t != n_blocks - 1)
    ext_ref[0:8, :] = jnp.where(has_prev, prev_ref[0], 0.0)
    ext_ref[8:8 + tb, :] = cur_ref[0]
    ext_ref[8 + tb:16 + tb, :] = jnp.where(has_next, next_ref[0], 0.0)
    acc = bias_ref[...] + coef_ref[0:1, :] * ext_ref[6:6 + tb, :]
    acc = acc + coef_ref[1:2, :] * ext_ref[7:7 + tb, :]
    acc = acc + coef_ref[2:3, :] * ext_ref[8:8 + tb, :]
    acc = acc + coef_ref[3:4, :] * ext_ref[9:9 + tb, :]
    o_ref[0] = acc


def local_mix(z, coef, bias, *, col0, tb, n_lat_blocks):
    bsz, t_len, _ = z.shape
    ncols = coef.shape[1]
    cw = 256
    assert ncols % cw == 0 and col0 % cw == 0
    nb = t_len // tb
    c0 = col0 // cw
    r8 = tb // 8
    return pl.pallas_call(
        functools.partial(_local_mix_body, tb=tb, n_lat_blocks=n_lat_blocks, n_blocks=nb),
        grid=(bsz, nb, ncols // cw),
        in_specs=[pl.BlockSpec((1, tb, cw), lambda b, t, c: (b, t, c + c0)),
                  pl.BlockSpec((1, 8, cw), lambda b, t, c: (b, jnp.maximum(t * r8 - 1, 0), c + c0)),
                  pl.BlockSpec((1, 8, cw), lambda b, t, c: (b, jnp.minimum((t + 1) * r8, nb * r8 - 1), c + c0)),
                  pl.BlockSpec((4, cw), lambda b, t, c: (0, c)),
                  pl.BlockSpec((1, cw), lambda b, t, c: (0, c))],
        out_specs=pl.BlockSpec((1, tb, cw), lambda b, t, c: (b, t, c)),
        out_shape=jax.ShapeDtypeStruct((bsz, t_len, ncols), F32),
        scratch_shapes=[pltpu.VMEM((tb + 16, cw), F32)],
        compiler_params=_cparams(("parallel", "parallel", "parallel")),
        name="local_mix",
    )(z, z, z, coef, bias.reshape(1, ncols))


def _lru_pre_body(x_ref, gaw_ref, gab_ref, gxw_ref, gxb_ref, lam_ref, a_ref, bin_ref):
    for d in range(2):
        lam = lam_ref[d:d + 1, :]
        sp = _softplus(-lam)
        for n in range(LRU_BLOCKS):
            sl = slice(n * LRU_BLOCK, (n + 1) * LRU_BLOCK)
            xs = x_ref[0, :, sl]
            r_gate = _sigmoid(_bdot(xs, gaw_ref[d, n]) + gab_ref[d:d + 1, sl])
            i_gate = _sigmoid(_bdot(xs, gxw_ref[d, n]) + gxb_ref[d:d + 1, sl])
            log_a = -LRU_C * r_gate * sp[:, sl]
            a = jnp.exp(log_a)
            a_ref[d, 0, :, sl] = a
            bin_ref[d, 0, :, sl] = jnp.sqrt(1.0 - jnp.exp(2.0 * log_a)) * (i_gate * xs)


def lru_pre(mixed, ga_w, ga_b, gx_w, gx_b, lam, *, tb):
    bsz, t_len, _ = mixed.shape
    w = LRU_BLOCKS * LRU_BLOCK
    full = lambda shape: pl.BlockSpec(shape, lambda b, t: (0,) * len(shape))
    out = jax.ShapeDtypeStruct((2, bsz, t_len, w), F32)
    ospec = pl.BlockSpec((2, 1, tb, w), lambda b, t: (0, b, t, 0))
    return pl.pallas_call(
        _lru_pre_body,
        grid=(bsz, t_len // tb),
        in_specs=[pl.BlockSpec((1, tb, w), lambda b, t: (b, t, 0)),
                  full(ga_w.shape), full(ga_b.shape), full(gx_w.shape), full(gx_b.shape), full(lam.shape)],
        out_specs=[ospec, ospec],
        out_shape=[out, out],
        compiler_params=_cparams(("parallel", "parallel")),
        name="lru_pre",
    )(mixed, ga_w, ga_b, gx_w, gx_b, lam)


def _group_scan(a, b, reverse):
    row = lax.broadcasted_iota(jnp.int32, a.shape, 0)
    for s in (1, 2, 4):
        if reverse:
            a_sh = pltpu.roll(a, 8 - s, axis=0)
            b_sh = pltpu.roll(b, 8 - s, axis=0)
            valid = row < 8 - s
        else:
            a_sh = pltpu.roll(a, s, axis=0)
            b_sh = pltpu.roll(b, s, axis=0)
            valid = row >= s
        b = jnp.where(valid, a * b_sh + b, b)
        a = jnp.where(valid, a * a_sh, a)
    return a, b


def _lru_scan_body(af_ref, bf_ref, ab_ref, bb_ref, hf_ref, hb_ref, cf_ref, cb_ref, *, tb):
    @pl.when(pl.program_id(2) == 0)
    def _():
        cf_ref[...] = jnp.zeros_like(cf_ref)
        cb_ref[...] = jnp.zeros_like(cb_ref)

    ng = tb // 8

    def step(g, carry):
        hf, hb = carry
        rf = pl.ds(pl.multiple_of(g * 8, 8), 8)
        a, b = _group_scan(af_ref[0, 0, rf, :], bf_ref[0, 0, rf, :], False)
        out = a * hf + b
        hf_ref[0, rf, :] = out
        hf = jnp.broadcast_to(out[7:8, :], out.shape)
        rb = pl.ds(pl.multiple_of((ng - 1 - g) * 8, 8), 8)
        a, b = _group_scan(ab_ref[0, 0, rb, :], bb_ref[0, 0, rb, :], True)
        out = a * hb + b
        hb_ref[0, rb, :] = out
        hb = jnp.broadcast_to(out[0:1, :], out.shape)
        return hf, hb

    hf, hb = lax.fori_loop(0, ng, step, (cf_ref[...], cb_ref[...]))
    cf_ref[...] = hf
    cb_ref[...] = hb


def lru_scan(a, bin_, *, tb, n_lat_blocks, n_ctx_blocks):
    _, bsz, t_len, w = a.shape
    cw = 512
    nb = t_len // tb
    fblk = lambda p: _scan_block(p, n_ctx_blocks, n_lat_blocks, False)
    bblk = lambda p: _scan_block(p, n_ctx_blocks, n_lat_blocks, True)
    fspec = pl.BlockSpec((1, 1, tb, cw), lambda b, c, p: (0, b, fblk(p), c))
    bspec = pl.BlockSpec((1, 1, tb, cw), lambda b, c, p: (1, b, bblk(p), c))
    out = jax.ShapeDtypeStruct((bsz, t_len, w), F32)
    return pl.pallas_call(
        functools.partial(_lru_scan_body, tb=tb),
        grid=(bsz, w // cw, nb),
        in_specs=[fspec, fspec, bspec, bspec],
        out_specs=[pl.BlockSpec((1, tb, cw), lambda b, c, p: (b, fblk(p), c)),
                   pl.BlockSpec((1, tb, cw), lambda b, c, p: (b, bblk(p), c))],
        out_shape=[out, out],
        scratch_shapes=[pltpu.VMEM((8, cw), F32), pltpu.VMEM((8, cw), F32)],
        compiler_params=_cparams(("parallel", "parallel", "arbitrary")),
        name="lru_scan",
    )(a, bin_, a, bin_)


def _head_sum_mats(width, head):
    e = np.zeros((width, 128), np.float32)
    e[np.arange(width), np.arange(width) // head] = 1.0
    return jnp.asarray(e), jnp.asarray(e.T.copy())


def _rwkv_prep_body(r_ref, k_ref, v_ref, wlo_ref, alo_ref, w0_ref, w2_ref, a0_ref, a2_ref,
                    kk_ref, ka_ref, rk_ref, e_ref, et_ref,
                    nkk_ref, lw_ref, kd_ref, bv_ref, bonus_ref):
    r = r_ref[0]
    k = k_ref[0]
    kkr = k * kk_ref[...]
    ssq = _hdot(kkr * kkr, e_ref[...])
    inv = 1.0 / jnp.maximum(jnp.sqrt(ssq), 1e-12)
    kk = kkr * _hdot(inv, et_ref[...])
    nkk_ref[0] = -kk
    tw = jnp.tanh(wlo_ref[0])
    al = alo_ref[0]
    rk = jnp.zeros_like(r)
    for d in range(2):
        w_log = -_softplus(-(w0_ref[d:d + 1, :] + _bdot(tw, w2_ref[d]))) - 0.5
        lw_ref[d, 0] = -jnp.exp(w_log)
        a_icl = _sigmoid(a0_ref[d:d + 1, :] + _bdot(al, a2_ref[d]))
        kd = k * (1.0 + (a_icl - 1.0) * ka_ref[...])
        kd_ref[d, 0] = kd
        bv_ref[d, 0] = kk * a_icl
        rk = rk + r * kd * rk_ref[...]
    bonus_ref[0] = _hdot(_hdot(rk, e_ref[...]), et_ref[...]) * v_ref[0]


def rwkv_prep(mixed, w0, w2, a0, a2, k_k, k_a, r_k, *, tb):
    bsz, t_len, _ = mixed.shape
    w = RWKV_HEADS * RWKV_HEAD
    e, et = _head_sum_mats(w, RWKV_HEAD)
    full = lambda shape: pl.BlockSpec(shape, lambda b, t: (0,) * len(shape))
    col = lambda width, idx: pl.BlockSpec((1, tb, width), lambda b, t: (b, t, idx))
    one = jax.ShapeDtypeStruct((bsz, t_len, w), F32)
    two = jax.ShapeDtypeStruct((2, bsz, t_len, w), F32)
    ospec1 = pl.BlockSpec((1, tb, w), lambda b, t: (b, t, 0))
    ospec2 = pl.BlockSpec((2, 1, tb, w), lambda b, t: (0, b, t, 0))
    return pl.pallas_call(
        _rwkv_prep_body,
        grid=(bsz, t_len // tb),
        in_specs=[col(w, 1), col(w, 2), col(w, 3), col(128, 4 * w // 128), col(128, 4 * w // 128 + 1),
                  full(w0.shape), full(w2.shape), full(a0.shape), full(a2.shape),
                  full((1, w)), full((1, w)), full((1, w)), full(e.shape), full(et.shape)],
        out_specs=[ospec1, ospec2, ospec2, ospec2, ospec1],
        out_shape=[one, two, two, two, one],
        compiler_params=_cparams(("parallel", "parallel")),
        name="rwkv_prep",
    )(mixed, mixed, mixed, mixed, mixed, w0, w2, a0, a2,
      k_k.reshape(1, w), k_a.reshape(1, w), r_k.reshape(1, w), e, et)


def _chunk_masks():
    idx = np.arange(CHUNK)
    t, s = idx[:, None], idx[None, :]
    tri = np.stack([(s <= t), (s >= t)]).astype(np.float32)
    lev = []
    l = 0
    while (1 << l) < CHUNK:
        lev.append(((t >> (l + 1)) == (s >> (l + 1))) & (((t >> l) & 1) != ((s >> l) & 1)))
        l += 1
    return jnp.asarray(tri), jnp.asarray(np.stack(lev).astype(np.float32))


def _rwkv_chunk_body(r_ref, v_ref, a_ref, lw_ref, kd_ref, bv_ref, tri_ref, lev_ref, y_ref, s_ref, *, heads):
    @pl.when(pl.program_id(3) == 0)
    def _():
        s_ref[...] = jnp.zeros_like(s_ref)

    tri = tri_ref[0]
    n = tri.shape[0]
    eye = (lax.broadcasted_iota(jnp.int32, (n, n), 0) == lax.broadcasted_iota(jnp.int32, (n, n), 1)).astype(F32)
    incl = tri > 0.0
    strict = (tri - eye) > 0.0
    n_lev = lev_ref.shape[0]
    for h in range(heads):
        sl = slice(h * RWKV_HEAD, (h + 1) * RWKV_HEAD)
        lw = lw_ref[0, 0, :, sl]
        r = r_ref[0, :, sl]
        v = v_ref[0, :, sl]
        a = a_ref[0, :, sl]
        k = kd_ref[0, 0, :, sl]
        b = bv_ref[0, 0, :, sl]
        g = _hdot(tri, lw)
        gt = jnp.sum(lw, axis=0, keepdims=True)
        at = a * jnp.exp(g - lw)
        rt = r * jnp.exp(g)
        eng = jnp.exp(-g)
        bt = b * eng
        kt = k * eng
        egt = jnp.exp(gt - g)
        m_ab = jnp.where(strict, _bdot_nt(at, bt), 0.0)
        m_ak = jnp.where(strict, _bdot_nt(at, kt), 0.0)
        m_rb = jnp.where(incl, _bdot_nt(rt, bt), 0.0)
        m_rk = jnp.where(incl, _bdot_nt(rt, kt), 0.0)
        tinv = eye + m_ab * lev_ref[0]
        for l in range(1, n_lev):
            tinv = tinv + _bdot(tinv, _bdot(m_ab * lev_ref[l], tinv))
        w1 = _bdot(tinv, at)
        w2 = _bdot(tinv, _bdot(m_ak, v))
        s0 = s_ref[h]
        u = _bdot_nt(w1, s0) + w2
        y_ref[0, 0, :, sl] = _bdot_nt(rt, s0) + _bdot(m_rb, u) + _bdot(m_rk, v)
        s_ref[h] = s0 * jnp.exp(gt) + _bdot_tn(u, b * egt) + _bdot_tn(v, k * egt)


def rwkv_scan(mixed, nkk, lw, kd, bv, *, n_ctx, heads=4):
    _, bsz, t_len, w = lw.shape
    gw = heads * RWKV_HEAD
    ncc, nlc = n_ctx // CHUNK, (t_len - n_ctx) // CHUNK
    tri, lev = _chunk_masks()

    def cn(d, c):
        return jnp.where(d == 0, _scan_block(c, ncc, nlc, False), _scan_block(c, ncc, nlc, True))

    c0 = w // gw
    m_spec = lambda off: pl.BlockSpec((1, CHUNK, gw), lambda d, b, hg, c: (b, cn(d, c), off + hg))
    d_spec = pl.BlockSpec((1, 1, CHUNK, gw), lambda d, b, hg, c: (d, b, cn(d, c), hg))
    return pl.pallas_call(
        functools.partial(_rwkv_chunk_body, heads=heads),
        grid=(2, bsz, w // gw, t_len // CHUNK),
        in_specs=[m_spec(c0), m_spec(3 * c0), m_spec(0), d_spec, d_spec, d_spec,
                  pl.BlockSpec((1, CHUNK, CHUNK), lambda d, b, hg, c: (d, 0, 0)),
                  pl.BlockSpec(lev.shape, lambda d, b, hg, c: (0, 0, 0))],
        out_specs=d_spec,
        out_shape=jax.ShapeDtypeStruct((2, bsz, t_len, w), F32),
        scratch_shapes=[pltpu.VMEM((heads, RWKV_HEAD, RWKV_HEAD), F32)],
        compiler_params=_cparams(("parallel", "parallel", "parallel", "arbitrary")),
        name="rwkv_scan",
    )(mixed, mixed, nkk, lw, kd, bv, tri, lev)


def _gelu_tanh(x):
    return 0.5 * x * (1.0 + jnp.tanh(0.7978845608028654 * (x + 0.044715 * x * x * x)))


def _even_post_body(hf_ref, hb_ref, gate_ref, y0_ref, y1_ref, bonus_ref, g_ref, lnw_ref, lnb_ref,
                    e_ref, et_ref, o_ref, *, width):
    o_ref[0, :, :width] = (hf_ref[0] + hb_ref[0]) * _gelu_tanh(gate_ref[0])
    y = y0_ref[0, 0] + y1_ref[0, 0]
    inv_n = 1.0 / RWKV_HEAD
    mean = _hdot(_hdot(y, e_ref[...]) * inv_n, et_ref[...])
    yc = y - mean
    var = _hdot(_hdot(yc * yc, e_ref[...]) * inv_n, et_ref[...])
    yn = yc * lax.rsqrt(var + RWKV_GN_EPS) * lnw_ref[...] + lnb_ref[...]
    o_ref[0, :, width:] = (yn + bonus_ref[0]) * g_ref[0]


def even_post(hs_f, hs_b, proj, y, bonus, g, ln_w, ln_b, *, tb):
    bsz, t_len, w = hs_f.shape
    e, et = _head_sum_mats(w, RWKV_HEAD)
    full = lambda shape: pl.BlockSpec(shape, lambda b, t: (0,) * len(shape))
    blk = pl.BlockSpec((1, tb, w), lambda b, t: (b, t, 0))
    dblk = lambda d: pl.BlockSpec((1, 1, tb, w), lambda b, t: (d, b, t, 0))
    return pl.pallas_call(
        functools.partial(_even_post_body, width=w),
        grid=(bsz, t_len // tb),
        in_specs=[blk, blk, blk, dblk(0), dblk(1), blk, blk, full((1, w)), full((1, w)),
                  full(e.shape), full(et.shape)],
        out_specs=pl.BlockSpec((1, tb, 2 * w), lambda b, t: (b, t, 0)),
        out_shape=jax.ShapeDtypeStruct((bsz, t_len, 2 * w), F32),
        compiler_params=_cparams(("parallel", "parallel")),
        name="even_post",
    )(hs_f, hs_b, proj, y, y, bonus, g, ln_w.reshape(1, w), ln_b.reshape(1, w), e, et)


def even_mixer_stream(h, n_ctx, w_in, w_out, conv_w, conv_b, ga_w, ga_b, gx_w, gx_b, lam,
                      mu, w0, w2, a0, a2, g2, k_k, k_a, r_k, ln_w, ln_b):
    bsz, t_len, d = h.shape
    rows = bsz * t_len
    lw_, rw_ = LRU_BLOCKS * LRU_BLOCK, RWKV_HEADS * RWKV_HEAD
    tb = min(256, n_ctx)
    n_lat_blocks, n_ctx_blocks = (t_len - n_ctx) // tb, n_ctx // tb
    n_main = 2 * lw_ + 3 * rw_ + 256
    h2 = h.reshape(rows, d)
    proj = matmul(h2, w_in, n_cols=n_main).reshape(bsz, t_len, n_main)
    n_g = w_in.shape[1] - n_main
    w_g = jnp.pad(w_in[:, n_main:], ((0, 0), (0, 512 - n_g)))
    g_lo = matmul(h2, w_g).reshape(bsz, t_len, 512)

    mu_main, mu_g = mu[:n_main - 2 * lw_], jnp.pad(mu[n_main - 2 * lw_:], (0, 512 - n_g))
    shift_taps = lambda m: jnp.stack([jnp.zeros_like(m), 0.5 * m, 1.0 - m, 0.5 * m])
    coef = jnp.concatenate([conv_w, shift_taps(mu_main)], axis=1)
    bias = jnp.concatenate([conv_b, jnp.zeros_like(mu_main)])
    mixed = local_mix(proj, coef, bias, col0=lw_, tb=tb, n_lat_blocks=n_lat_blocks)
    g_mix = local_mix(g_lo, shift_taps(mu_g), jnp.zeros_like(mu_g), col0=0, tb=tb, n_lat_blocks=n_lat_blocks)

    a, bin_ = lru_pre(mixed, ga_w, ga_b, gx_w, gx_b, lam, tb=min(128, tb))
    hs_f, hs_b = lru_scan(a, bin_, tb=tb, n_lat_blocks=n_lat_blocks, n_ctx_blocks=n_ctx_blocks)

    nkk, lw, kd, bv, bonus = rwkv_prep(mixed, w0, w2, a0, a2, k_k, k_a, r_k, tb=min(128, tb))
    y = rwkv_scan(mixed, nkk, lw, kd, bv, n_ctx=n_ctx)
    g2p = jnp.pad(g2, ((0, 512 - n_g), (0, 0)))
    g = matmul(g_mix.reshape(rows, 512), g2p, act="sigmoid").reshape(bsz, t_len, rw_)

    merged = even_post(hs_f, hs_b, proj, y, bonus, g, ln_w, ln_b, tb=min(128, tb))
    return matmul(merged.reshape(rows, 2 * rw_), w_out).reshape(bsz, t_len, d)


def _gla_chunk_body(q_ref, k_ref, v_ref, gklo_ref, gkw_ref, gkb_ref, tri_ref, o_ref, s_ref, *, dk):
    @pl.when(pl.program_id(3) == 0)
    def _():
        s_ref[...] = jnp.zeros_like(s_ref)

    tri = tri_ref[0]
    gk = _bdot(gklo_ref[0], gkw_ref[0]) + gkb_ref[0]
    la = jnp.maximum(-_softplus(-gk) / GLA_GATE_NORM, GLA_CLAMP_MIN)
    cum = _hdot(tri, la)
    tot = jnp.sum(la, axis=0, keepdims=True)
    k = k_ref[0]
    v = v_ref[0]
    qd = q_ref[0] * (dk ** -0.5) * jnp.exp(cum)
    scores = jnp.where(tri > 0.0, _bdot_nt(qd, k * jnp.exp(-cum)), 0.0)
    s0 = s_ref[...]
    o_ref[0, 0] = _bdot(scores, v) + _bdot_nt(qd, s0)
    s_ref[...] = s0 * jnp.exp(tot) + _bdot_tn(v, k * jnp.exp(tot - cum))


def gla_scan(proj, gk_lo, gk_w, gk_b, *, n_ctx, key_w, val_w):
    bsz, t_len, _ = proj.shape
    dk, dv = key_w // GLA_HEADS, val_w // GLA_HEADS
    ncc, nlc = n_ctx // CHUNK, (t_len - n_ctx) // CHUNK
    tri, _ = _chunk_masks()

    def cn(d, c):
        return jnp.where(d == 0, _scan_block(c, ncc, nlc, False), _scan_block(c, ncc, nlc, True))

    return pl.pallas_call(
        functools.partial(_gla_chunk_body, dk=dk),
        grid=(2, bsz, GLA_HEADS, t_len // CHUNK),
        in_specs=[pl.BlockSpec((1, CHUNK, dk), lambda d, b, h, c: (b, cn(d, c), h)),
                  pl.BlockSpec((1, CHUNK, dk), lambda d, b, h, c: (b, cn(d, c), GLA_HEADS + h)),
                  pl.BlockSpec((1, CHUNK, dv), lambda d, b, h, c: (b, cn(d, c), 2 * key_w // dv + h)),
                  pl.BlockSpec((1, CHUNK, 128), lambda d, b, h, c: (b, cn(d, c), 0)),
                  pl.BlockSpec((1, 128, dk), lambda d, b, h, c: (d, 0, h)),
                  pl.BlockSpec((1, 1, dk), lambda d, b, h, c: (d, 0, h)),
                  pl.BlockSpec((1, CHUNK, CHUNK), lambda d, b, h, c: (d, 0, 0))],
        out_specs=pl.BlockSpec((1, 1, CHUNK, dv), lambda d, b, h, c: (d, b, cn(d, c), h)),
        out_shape=jax.ShapeDtypeStruct((2, bsz, t_len, val_w), F32),
        scratch_shapes=[pltpu.VMEM((dv, dk), F32)],
        compiler_params=_cparams(("parallel", "parallel", "parallel", "arbitrary")),
        name="gla_scan",
    )(proj, proj, proj, gk_lo, gk_w, gk_b, tri)


def _odd_post_body(o0_ref, o1_ref, gate_ref, nw_ref, out_ref, *, dv):
    for h in range(GLA_HEADS):
        sl = slice(h * dv, (h + 1) * dv)
        o = o0_ref[0, 0, :, sl] + o1_ref[0, 0, :, sl]
        o = o * lax.rsqrt(jnp.mean(o * o, axis=-1, keepdims=True) + GLA_EPS) * nw_ref[...]
        gate = gate_ref[0, :, sl]
        out_ref[0, :, sl] = o * (gate * _sigmoid(gate))


def odd_post(o, proj, norm_w, *, tb, gate_block):
    _, bsz, t_len, w = o.shape
    dv = w // GLA_HEADS
    dblk = lambda d: pl.BlockSpec((1, 1, tb, w), lambda b, t: (d, b, t, 0))
    return pl.pallas_call(
        functools.partial(_odd_post_body, dv=dv),
        grid=(bsz, t_len // tb),
        in_specs=[dblk(0), dblk(1), pl.BlockSpec((1, tb, w), lambda b, t: (b, t, gate_block)),
                  pl.BlockSpec((1, dv), lambda b, t: (0, 0))],
        out_specs=pl.BlockSpec((1, tb, w), lambda b, t: (b, t, 0)),
        out_shape=jax.ShapeDtypeStruct((bsz, t_len, w), F32),
        compiler_params=_cparams(("parallel", "parallel")),
        name="odd_post",
    )(o, o, proj, norm_w.reshape(1, dv))


def odd_mixer_stream(h, n_ctx, w_in, w_out, gk_w2, gk_b, norm_w):
    bsz, t_len, d = h.shape
    rows = bsz * t_len
    key_w = gk_w2.shape[-1]
    val_w = w_out.shape[0]
    n_main = 2 * key_w + 2 * val_w
    h2 = h.reshape(rows, d)
    proj = matmul(h2, w_in, n_cols=n_main).reshape(bsz, t_len, n_main)
    n_lo = w_in.shape[1] - n_main
    gk_lo = matmul(h2, jnp.pad(w_in[:, n_main:], ((0, 0), (0, 128 - n_lo)))).reshape(bsz, t_len, 128)
    gk_w = jnp.zeros((2, 128, key_w), F32)
    for dd in range(2):
        gk_w = gk_w.at[dd, dd * GLA_GATE_LORA:(dd + 1) * GLA_GATE_LORA].set(gk_w2[dd])
    o = gla_scan(proj, gk_lo, gk_w, gk_b.reshape(2, 1, key_w), n_ctx=n_ctx, key_w=key_w, val_w=val_w)
    merged = odd_post(o, proj, norm_w, tb=min(128, n_ctx), gate_block=(2 * key_w + val_w) // val_w)
    return matmul(merged.reshape(rows, val_w), w_out).reshape(bsz, t_len, d)


def _router_body(f_ref, rw_ref, rb_ref, ltri_ref, info_ref, cnt_ref, carry_ref):
    @pl.when(pl.program_id(0) == 0)
    def _():
        carry_ref[...] = jnp.zeros_like(carry_ref)

    logits = _hdot(f_ref[...], rw_ref[...]) + rb_ref[...]
    lane = lax.broadcasted_iota(jnp.int32, logits.shape, 1)
    lanef = lane.astype(F32)
    work = logits
    sel = jnp.zeros_like(logits)
    vals, hots = [], []
    info = jnp.zeros_like(logits)
    for k in range(TOP_K):
        m = jnp.max(work, axis=-1, keepdims=True)
        idx = jnp.min(jnp.where(work == m, lanef, 128.0), axis=-1, keepdims=True)
        hot = lanef == idx
        vals.append(m)
        hots.append(hot)
        work = jnp.where(hot, -3e38, work)
        sel = sel + hot.astype(F32)
        info = info + jnp.where(lane == k, idx, 0.0)
    exps = [jnp.exp(v - vals[0]) for v in vals]
    den = exps[0] + exps[1] + exps[2] + exps[3]
    rank_mat = _bdot(ltri_ref[...], sel) + carry_ref[...]
    for k in range(TOP_K):
        info = info + jnp.where(lane == TOP_K + k, exps[k] / den, 0.0)
        rank = jnp.sum(jnp.where(hots[k], rank_mat, 0.0), axis=-1, keepdims=True)
        info = info + jnp.where(lane == 2 * TOP_K + k, rank, 0.0)
    info_ref[...] = info
    carry = carry_ref[...] + jnp.sum(sel, axis=0, keepdims=True)
    carry_ref[...] = carry
    cnt_ref[...] = carry


def moe_route(f, router_w, router_b):
    n, d = f.shape
    n_exp = router_w.shape[1]
    tm = 256
    while n % tm:
        tm //= 2
    rw = jnp.pad(router_w, ((0, 0), (0, 128 - n_exp)))
    rb = jnp.pad(router_b, (0, 128 - n_exp), constant_values=-1e30).reshape(1, 128)
    ltri = jnp.asarray(np.tril(np.ones((tm, tm), np.float32), -1))
    return pl.pallas_call(
        _router_body,
        grid=(n // tm,),
        in_specs=[pl.BlockSpec((tm, d), lambda i: (i, 0)), pl.BlockSpec((d, 128), lambda i: (0, 0)),
                  pl.BlockSpec((1, 128), lambda i: (0, 0)), pl.BlockSpec((tm, tm), lambda i: (0, 0))],
        out_specs=[pl.BlockSpec((tm, 128), lambda i: (i, 0)), pl.BlockSpec((1, 128), lambda i: (0, 0))],
        out_shape=[jax.ShapeDtypeStruct((n, 128), F32), jax.ShapeDtypeStruct((1, 128), F32)],
        scratch_shapes=[pltpu.VMEM((1, 128), F32)],
        compiler_params=_cparams(("arbitrary",)),
        name="moe_route",
    )(f, rw, rb, ltri)


def _row_tok_body(pos_ref, out_ref, *, n_tok, n_rows):
    def fill(i, c):
        out_ref[i] = n_tok - 1
        return c

    lax.fori_loop(0, n_rows, fill, 0)

    def put(i, c):
        out_ref[pos_ref[i]] = i // TOP_K
        return c

    lax.fori_loop(0, n_tok * TOP_K, put, 0)


def moe_row_tokens(pos_flat, n_tok, n_rows):
    return pl.pallas_call(
        functools.partial(_row_tok_body, n_tok=n_tok, n_rows=n_rows),
        in_specs=[pl.BlockSpec(memory_space=pltpu.SMEM)],
        out_specs=pl.BlockSpec(memory_space=pltpu.SMEM),
        out_shape=jax.ShapeDtypeStruct((n_rows,), jnp.int32),
        name="moe_row_tokens",
    )(pos_flat)


def _expert_gu_body(rt_ref, be_ref, nu_ref, x_hbm, w_ref, b_ref, o_ref, xbuf, sem, *, f_dim):
    i = pl.program_id(0)

    def row_copy(r, tok):
        return pltpu.make_async_copy(x_hbm.at[pl.ds(tok, 1)], xbuf.at[pl.ds(r, 1)], sem)

    @pl.when(i < nu_ref[0])
    def _():
        def issue(r, c):
            row_copy(r, rt_ref[i * MOE_BLOCK + r]).start()
            return c

        lax.fori_loop(0, MOE_BLOCK, issue, 0)

        def wait(r, c):
            row_copy(r, 0).wait()
            return c

        lax.fori_loop(0, MOE_BLOCK, wait, 0)
        hid = _bdot(xbuf[...], w_ref[0]) + b_ref[0]
        glu = jnp.minimum(hid[:, :f_dim], SWIGLU_LIMIT)
        lin = jnp.clip(hid[:, f_dim:], -SWIGLU_LIMIT, SWIGLU_LIMIT)
        o_ref[...] = glu * _sigmoid(SWIGLU_ALPHA * glu) * (lin + 1.0)

    @pl.when(i >= nu_ref[0])
    def _():
        o_ref[...] = jnp.zeros_like(o_ref)


def moe_expert_gu(f, row_tok, block_e, n_used, w_gu, b_gu):
    n, d = f.shape
    n_exp, _, f2 = w_gu.shape
    n_blocks = block_e.shape[0]
    grid_spec = pltpu.PrefetchScalarGridSpec(
        num_scalar_prefetch=3,
        grid=(n_blocks,),
        in_specs=[pl.BlockSpec(memory_space=pl.ANY),
                  pl.BlockSpec((1, d, f2), lambda i, rt, be, nu: (be[i], 0, 0)),
                  pl.BlockSpec((1, 1, f2), lambda i, rt, be, nu: (be[i], 0, 0))],
        out_specs=pl.BlockSpec((MOE_BLOCK, f2 // 2), lambda i, rt, be, nu: (i, 0)),
        scratch_shapes=[pltpu.VMEM((MOE_BLOCK, d), F32), pltpu.SemaphoreType.DMA(())],
    )
    return pl.pallas_call(
        functools.partial(_expert_gu_body, f_dim=f2 // 2),
        grid_spec=grid_spec,
        out_shape=jax.ShapeDtypeStruct((n_blocks * MOE_BLOCK, f2 // 2), F32),
        compiler_params=_cparams(("arbitrary",)),
        name="moe_expert_gu",
    )(row_tok, block_e, n_used, f, w_gu, b_gu.reshape(n_exp, 1, f2))


def _expert_down_body(be_ref, nu_ref, a_ref, w_ref, b_ref, o_ref):
    @pl.when(pl.program_id(0) < nu_ref[0])
    def _():
        o_ref[...] = _bdot(a_ref[...], w_ref[0]) + b_ref[0]

    @pl.when(pl.program_id(0) >= nu_ref[0])
    def _():
        o_ref[...] = jnp.zeros_like(o_ref)


def moe_expert_down(act, block_e, n_used, w_down, b_down):
    n_exp, f_dim, d = w_down.shape
    n_blocks = block_e.shape[0]
    grid_spec = pltpu.PrefetchScalarGridSpec(
        num_scalar_prefetch=2,
        grid=(n_blocks,),
        in_specs=[pl.BlockSpec((MOE_BLOCK, f_dim), lambda i, be, nu: (i, 0)),
                  pl.BlockSpec((1, f_dim, d), lambda i, be, nu: (be[i], 0, 0)),
                  pl.BlockSpec((1, 1, d), lambda i, be, nu: (be[i], 0, 0))],
        out_specs=pl.BlockSpec((MOE_BLOCK, d), lambda i, be, nu: (i, 0)),
    )
    return pl.pallas_call(
        _expert_down_body,
        grid_spec=grid_spec,
        out_shape=jax.ShapeDtypeStruct((n_blocks * MOE_BLOCK, d), F32),
        compiler_params=_cparams(("arbitrary",)),
        name="moe_expert_down",
    )(block_e, n_used, act, w_down, b_down.reshape(n_exp, 1, d))


def _combine_body(pos_ref, y_hbm, info_ref, xl_ref, xc_ref, gate_ref, *rest, tc, n_lat_blocks, nt,
                  with_ctx, final):
    if final:
        gain_ref, rest = rest[0], rest[1:]
    xln_ref = rest[0]
    buf, sem = rest[-2], rest[-1]
    b, t = pl.program_id(0), pl.program_id(1)
    base = (b * nt + t) * tc * TOP_K

    def row_copy(j, row):
        return pltpu.make_async_copy(y_hbm.at[pl.ds(row, 1)], buf.at[j % TOP_K, pl.ds(j // TOP_K, 1)], sem)

    def issue(j, c):
        row_copy(j, pos_ref[base + j]).start()
        return c

    lax.fori_loop(0, tc * TOP_K, issue, 0)

    def wait(j, c):
        row_copy(j, 0).wait()
        return c

    lax.fori_loop(0, tc * TOP_K, wait, 0)
    info = info_ref[...]
    moe = info[:, TOP_K:TOP_K + 1] * buf[0]
    for k in range(1, TOP_K):
        moe = moe + info[:, TOP_K + k:TOP_K + k + 1] * buf[k]
    is_ctx = t >= n_lat_blocks
    x = jnp.where(is_ctx, xc_ref[0], xl_ref[0]) + gate_ref[0] * moe
    if final:
        x = x * lax.rsqrt(jnp.mean(x * x, axis=-1, keepdims=True) + NORM_EPS) * gain_ref[...]
    if with_ctx:
        xcn_ref = rest[1]

        @pl.when(is_ctx)
        def _():
            xcn_ref[0] = x

        @pl.when(jnp.logical_not(is_ctx))
        def _():
            xln_ref[0] = x
    else:
        xln_ref[0] = x


def moe_combine(y, pos_flat, info, x_lat, x_ctx, gate, *, tc, col_major, with_ctx, final_gain=None):
    bsz, seq, d = x_lat.shape
    n_ctx = x_ctx.shape[1]
    nl, nc, lat_spec, ctx_spec = _token_specs(bsz, seq, n_ctx, d, tc, col_major)
    nt = nl + (nc if with_ctx else 0)
    final = final_gain is not None
    xl = x_lat.reshape(bsz, seq // GRID_W, GRID_W * d) if col_major else x_lat
    in_specs = [pl.BlockSpec(memory_space=pl.ANY),
                pl.BlockSpec((tc, 128), lambda b, t, *_: (b * nt + t, 0)),
                lat_spec, ctx_spec, _mod_spec(d, nl)]
    args = [y, info, xl, x_ctx, gate]
    if final:
        in_specs.append(pl.BlockSpec((1, d), lambda b, t, *_: (0, 0)))
        args.append(final_gain.reshape(1, d))
    out_specs = [lat_spec] + ([ctx_spec] if with_ctx else [])
    out_shape = [jax.ShapeDtypeStruct(xl.shape, F32)] + ([jax.ShapeDtypeStruct(x_ctx.shape, F32)] if with_ctx else [])
    grid_spec = pltpu.PrefetchScalarGridSpec(
        num_scalar_prefetch=1,
        grid=(bsz, nt),
        in_specs=in_specs,
        out_specs=out_specs,
        scratch_shapes=[pltpu.VMEM((TOP_K, tc, d), F32), pltpu.SemaphoreType.DMA(())],
    )
    outs = pl.pallas_call(
        functools.partial(_combine_body, tc=tc, n_lat_blocks=nl, nt=nt, with_ctx=with_ctx, final=final),
        grid_spec=grid_spec,
        out_shape=out_shape,
        compiler_params=_cparams(("arbitrary", "arbitrary")),
        name="moe_combine",
    )(pos_flat, *args)
    return outs[0].reshape(x_lat.shape), (outs[1] if with_ctx else None)


def moe_layer(f, x_lat, x_ctx, gate, router_w, router_b, w_gu, b_gu, w_down, b_down, *,
              tc, col_major, with_ctx, final_gain=None):
    n, d = f.shape
    n_exp = router_w.shape[1]
    info, cnt = moe_route(f, router_w, router_b)
    e_idx = info[:, :TOP_K].astype(jnp.int32)
    rank = info[:, 2 * TOP_K:3 * TOP_K].astype(jnp.int32)
    counts = cnt[0, :n_exp].astype(jnp.int32)
    padded = (counts + MOE_BLOCK - 1) // MOE_BLOCK * MOE_BLOCK
    pad_end = jnp.cumsum(padded)
    pos_flat = ((pad_end - padded)[e_idx] + rank).reshape(-1)
    n_blocks = -(-(n * TOP_K) // MOE_BLOCK) + n_exp
    block_start = jnp.arange(n_blocks, dtype=jnp.int32) * MOE_BLOCK
    block_e = jnp.minimum(jnp.sum(block_start[:, None] >= pad_end[None, :], axis=1), n_exp - 1).astype(jnp.int32)
    n_used = (pad_end[-1:] // MOE_BLOCK).astype(jnp.int32)
    row_tok = moe_row_tokens(pos_flat, n, n_blocks * MOE_BLOCK)
    act = moe_expert_gu(f, row_tok, block_e, n_used, w_gu, b_gu)
    y = moe_expert_down(act, block_e, n_used, w_down, b_down)
    return moe_combine(y, pos_flat, info, x_lat, x_ctx, gate, tc=tc, col_major=col_major,
                       with_ctx=with_ctx, final_gain=final_gain)


def kernel(x, c, ctx, c_ctx, ada_w, ada_b, norm_mix, norm_ffn, norm_final, ev_w_in, ev_w_out, lru_conv_w, lru_conv_b, lru_gate_a_w, lru_gate_a_b, lru_gate_x_w, lru_gate_x_b, lru_lambda, rwkv_mu, rwkv_w0, rwkv_w2, rwkv_a0, rwkv_a2, rwkv_g2, rwkv_k_k, rwkv_k_a, rwkv_r_k, rwkv_ln_w, rwkv_ln_b, od_w_in, od_w_out, gla_gk_w2, gla_gk_b, gla_norm_w, router_w, router_b, exp_w_gu, exp_b_gu, exp_w_down, exp_b_down):
    bsz, seq, d = x.shape
    n_ctx = ctx.shape[1]
    depth = ada_w.shape[0]
    rows = seq // GRID_W
    cond = jnp.zeros((8, d), F32).at[:bsz].set(c).at[bsz].set(c_ctx)
    x_lat, x_ctx = x, ctx
    for i in range(depth):
        last = i == depth - 1
        j = i // 2
        mod = ada_mod(cond, ada_w[i], ada_b[i])
        mods = [mod[:bsz + 1, k * d:(k + 1) * d].reshape(bsz + 1, 1, d) for k in range(N_MOD)]
        even = i % 2 == 0
        rb = min(128, n_ctx) if even else rows
        h = norm_mod(x_lat, x_ctx, norm_mix[i], mods[0], mods[1], rb=rb, col_major=not even)
        if even:
            y = even_mixer_stream(h, n_ctx, ev_w_in[j], ev_w_out[j], lru_conv_w[j], lru_conv_b[j],
                                  lru_gate_a_w[j], lru_gate_a_b[j], lru_gate_x_w[j], lru_gate_x_b[j],
                                  lru_lambda[j], rwkv_mu[j], rwkv_w0[j], rwkv_w2[j], rwkv_a0[j],
                                  rwkv_a2[j], rwkv_g2[j], rwkv_k_k[j], rwkv_k_a[j], rwkv_r_k[j].reshape(-1),
                                  rwkv_ln_w[j], rwkv_ln_b[j])
        else:
            y = odd_mixer_stream(h, n_ctx, od_w_in[j], od_w_out[j], gla_gk_w2[j], gla_gk_b[j], gla_norm_w[j])
        with_ctx = not last
        x_lat, x_ctx_new, f = resid_norm_mod(x_lat, x_ctx, y, mods[2], norm_ffn[i], mods[3], mods[4],
                                             rb=rb, col_major=not even, with_ctx=with_ctx)
        if with_ctx:
            x_ctx = x_ctx_new
        x_lat, x_ctx_new = moe_layer(f.reshape(-1, d), x_lat, x_ctx, mods[5], router_w[i], router_b[i],
                                     exp_w_gu[i], exp_b_gu[i], exp_w_down[i], exp_b_down[i],
                                     tc=min(64, rb), col_major=not even, with_ctx=with_ctx,
                                     final_gain=norm_final if last else None)
        if with_ctx:
            x_ctx = x_ctx_new
    return x_lat
```
